```python
import jax, jax.numpy as jnp
from jax import lax
import numpy as np

D_MODEL = 1024
BATCH = 2
SEQ = 16384
DEPTH = 4

GRID_W = 64
CTX_LEN = 256
HEAD_DIM = 64
NORM_EPS = 1e-6
POOL_WIDTH = D_MODEL // 4
POOL_WINDOWS = (2, 4, 8, 16)
POOL_GROUP = POOL_WIDTH // 4
GLA_WIDTH = D_MODEL // 4
GLA_DK = 64
GLA_DV = 64
GLA_HEADS = GLA_WIDTH // GLA_DV
GLA_GATE_RANK = 16
GLA_GATE_NORM = 16.0
GLA_CHUNK = 64
ATTN_WIDTH = D_MODEL // 2
ATTN_Q_HEADS = ATTN_WIDTH // HEAD_DIM
ATTN_KV_HEADS = ATTN_Q_HEADS // 4
ATTN_KV_WIDTH = ATTN_KV_HEADS * HEAD_DIM
Q_BLOCK = 128
ROPE_THETA = 10000.0
MLP_HIDDEN = 4 * D_MODEL
IN_SIZES = (POOL_WIDTH,
            GLA_HEADS * GLA_DK, GLA_HEADS * GLA_DK,
            GLA_WIDTH, GLA_WIDTH,
            GLA_GATE_RANK, GLA_GATE_RANK,
            ATTN_WIDTH, ATTN_KV_WIDTH, ATTN_KV_WIDTH)
IN_WIDTH = sum(IN_SIZES)
MIX_WIDTH = POOL_WIDTH + GLA_WIDTH + ATTN_WIDTH

kernel_name = "hybrid_pool_gla_gqa_prefix_dit"


def rmsnorm(x, g):
    xf = x.astype(jnp.float32)
    y = xf * lax.rsqrt(jnp.mean(xf * xf, axis=-1, keepdims=True) + NORM_EPS)
    return (y * g).astype(x.dtype)


def modulation(cvec, w_mod, b_mod):
    m = (jax.nn.silu(cvec) @ w_mod + b_mod)[..., None, :]
    return jnp.split(m, 6, axis=-1)


def split_in(u):
    idx = np.cumsum(np.array(IN_SIZES))[:-1].tolist()
    return jnp.split(u, idx, axis=-1)


def pool_mixer(p, w_pool, s_pool):
    B, N, _ = p.shape
    pg = p.reshape(B, N, 4, POOL_GROUP)
    pf = pg.astype(jnp.float32)
    P = jnp.concatenate([jnp.zeros((B, 1, 4, POOL_GROUP), jnp.float32), jnp.cumsum(pf, axis=1)], axis=1)
    t = np.arange(N)
    outs = []
    for gi, w in enumerate(POOL_WINDOWS):
        lo = np.clip(t - w // 2, 0, N - 1)
        hi = np.clip(t + w // 2 - 1, 0, N - 1)
        cnt = jnp.asarray((hi - lo + 1)[None, :, None], jnp.float32)
        Pg = P[:, :, gi]
        mean = (Pg[:, hi + 1] - Pg[:, lo]) / cnt
        outs.append(mean - pf[:, :, gi])
    m = jnp.stack(outs, axis=2)
    y = jnp.einsum('bngc,gcd->bngd', m, w_pool.astype(jnp.float32))
    return (y.reshape(B, N, POOL_WIDTH) * s_pool).astype(p.dtype)


def gla_features(q, k, v, r_f, r_b, w_gate, b_gate):
    B, N, _ = q.shape
    f32 = jnp.float32
    heads = lambda a: a.astype(f32).reshape(B, N, GLA_HEADS, -1)
    wg = w_gate.astype(f32)
    bg = b_gate.astype(f32)
    la_f = jax.nn.log_sigmoid(r_f.astype(f32) @ wg[0] + bg[0]) / GLA_GATE_NORM
    la_b = jax.nn.log_sigmoid(r_b.astype(f32) @ wg[1] + bg[1]) / GLA_GATE_NORM
    return heads(q) * GLA_DK ** -0.5, heads(k), heads(v), heads(la_f), heads(la_b)


def gla_chunk_scan(q, k, v, la, s0, strict):
    B, N, H, _ = q.shape
    C = GLA_CHUNK
    nc = N // C
    to_chunks = lambda a: a.reshape(B, nc, C, H, a.shape[-1]).transpose(1, 0, 3, 2, 4)
    mask = np.tril(np.ones((C, C), bool), k=-1 if strict else 0)[:, :, None]

    def step(S, inp):
        qc, kc, vc, lc = inp
        b = jnp.cumsum(lc, axis=2)
        o_inter = jnp.einsum('bhid,bhde->bhie', qc * jnp.exp(b), S)
        diff = b[:, :, :, None, :] - b[:, :, None, :, :]
        decay = jnp.exp(jnp.where(mask, diff, -jnp.inf))
        A = jnp.einsum('bhid,bhjd,bhijd->bhij', qc, kc, decay)
        o = o_inter + jnp.einsum('bhij,bhje->bhie', A, vc)
        bl = b[:, :, -1:, :]
        S_new = jnp.exp(bl[:, :, 0, :, None]) * S + jnp.einsum('bhjd,bhje->bhde', kc * jnp.exp(bl - b), vc)
        return S_new, o

    S, o = lax.scan(step, s0, (to_chunks(q), to_chunks(k), to_chunks(v), to_chunks(la)))
    return o.transpose(1, 0, 3, 2, 4).reshape(B, N, H, -1), S


def gla_bidir(q, k, v, la_f, la_b, s_f, s_b):
    o_f, s_f_new = gla_chunk_scan(q, k, v, la_f, s_f, strict=False)
    rev = lambda a: a[:, ::-1]
    o_b, s_b_new = gla_chunk_scan(rev(q), rev(k), rev(v), rev(la_b), s_b, strict=True)
    return o_f + rev(o_b), s_f_new, s_b_new


def gla_output(o, g, g_gla, out_dtype):
    B, N = o.shape[:2]
    gate = jax.nn.silu(g.astype(jnp.float32)).reshape(B, N, GLA_HEADS, GLA_DV)
    return (rmsnorm(o, g_gla) * gate).reshape(B, N, GLA_WIDTH).astype(out_dtype)


def attn_qkv(q, k, v, g_q, g_k):
    B, N, _ = q.shape
    q = rmsnorm(q.reshape(B, N, ATTN_Q_HEADS, HEAD_DIM), g_q)
    k = rmsnorm(k.reshape(B, N, ATTN_KV_HEADS, HEAD_DIM), g_k)
    return q, k, v.reshape(B, N, ATTN_KV_HEADS, HEAD_DIM)


def axial_rope(x, rows, cols):
    half = HEAD_DIM // 2
    inv = ROPE_THETA ** (-jnp.arange(0, half, 2, dtype=jnp.float32) / half)

    def rot(xh, pos):
        ang = pos.astype(jnp.float32)[:, None] * inv
        cos = jnp.cos(ang)[None, :, None, :]
        sin = jnp.sin(ang)[None, :, None, :]
        x1, x2 = jnp.split(xh.astype(jnp.float32), 2, axis=-1)
        return jnp.concatenate([x1 * cos - x2 * sin, x1 * sin + x2 * cos], axis=-1)

    xr, xc = jnp.split(x, 2, axis=-1)
    return jnp.concatenate([rot(xr, rows), rot(xc, cols)], axis=-1).astype(x.dtype)


def gqa_attend(qb, k, v):
    s = jnp.einsum('bqhgd,bkhd->bhgqk', qb, k, preferred_element_type=jnp.float32) * HEAD_DIM ** -0.5
    p = jax.nn.softmax(s, axis=-1)
    return jnp.einsum('bhgqk,bkhd->bqhgd', p.astype(v.dtype), v)


def latent_attention(q, k_all, v_all):
    B, N = q.shape[:2]
    G = ATTN_Q_HEADS // ATTN_KV_HEADS
    nb = N // Q_BLOCK
    qb = q.reshape(B, nb, Q_BLOCK, ATTN_KV_HEADS, G, HEAD_DIM).swapaxes(0, 1)
    ob = lax.map(lambda blk: gqa_attend(blk, k_all, v_all), qb)
    return ob.swapaxes(0, 1).reshape(B, N, ATTN_WIDTH)


def context_attention(q, k, v):
    B, L = q.shape[:2]
    G = ATTN_Q_HEADS // ATTN_KV_HEADS
    return gqa_attend(q.reshape(B, L, ATTN_KV_HEADS, G, HEAD_DIM), k, v).reshape(B, L, ATTN_WIDTH)


def squared_relu_mlp(h, w_up, w_down):
    return jnp.square(jax.nn.relu(h @ w_up)) @ w_down


def trunk_layer(x, cx, c, c_ctx, rows, cols, w_mod, b_mod, g_mix, w_in, w_pool, s_pool, w_gate, b_gate,
                g_gla, g_q, g_k, w_out, g_mlp, w_up, w_down, update_ctx):
    sh1, sc1, ga1, sh2, sc2, ga2 = modulation(c, w_mod, b_mod)
    sh1c, sc1c, ga1c, sh2c, sc2c, ga2c = modulation(c_ctx[None], w_mod, b_mod)

    u = (rmsnorm(x, g_mix) * (1 + sc1) + sh1) @ w_in
    uc = (rmsnorm(cx, g_mix) * (1 + sc1c) + sh1c) @ w_in
    p, gq, gk, gv, gg, rf, rb, aq, ak, av = split_in(u)
    pc, gqc, gkc, gvc, ggc, rfc, rbc, aqc, akc, avc = split_in(uc)

    B = x.shape[0]
    s_zero = jnp.zeros((B, GLA_HEADS, GLA_DK, GLA_DV), jnp.float32)
    o_gc, s_f, s_b = gla_bidir(*gla_features(gqc, gkc, gvc, rfc, rbc, w_gate, b_gate), s_zero, s_zero)
    o_gl, _, _ = gla_bidir(*gla_features(gq, gk, gv, rf, rb, w_gate, b_gate), s_f, s_b)

    qc_, kc_, vc_ = attn_qkv(aqc, akc, avc, g_q, g_k)
    ql, kl, vl = attn_qkv(aq, ak, av, g_q, g_k)
    ql = axial_rope(ql, rows, cols)
    kl = axial_rope(kl, rows, cols)
    k_all = jnp.concatenate([kc_, kl], axis=1)
    v_all = jnp.concatenate([vc_, vl], axis=1)

    mix = jnp.concatenate([pool_mixer(p, w_pool, s_pool),
                           gla_output(o_gl, gg, g_gla, x.dtype),
                           latent_attention(ql, k_all, v_all)], axis=-1) @ w_out
    x = x + ga1 * mix
    x = x + ga2 * squared_relu_mlp(rmsnorm(x, g_mlp) * (1 + sc2) + sh2, w_up, w_down)

    if update_ctx:
        mix_c = jnp.concatenate([pool_mixer(pc, w_pool, s_pool),
                                 gla_output(o_gc, ggc, g_gla, cx.dtype),
                                 context_attention(qc_, kc_, vc_)], axis=-1) @ w_out
        cx = cx + ga1c * mix_c
        cx = cx + ga2c * squared_relu_mlp(rmsnorm(cx, g_mlp) * (1 + sc2c) + sh2c, w_up, w_down)
    return x, cx


def setup_inputs(seed: int = 0) -> dict:
    key = jax.random.key(seed)
    ks = jax.random.split(key, 20)
    D = D_MODEL
    nrm = lambda k, shape, scale: jax.random.normal(k, shape, jnp.float32) * scale
    return {
        "x": nrm(ks[0], (BATCH, SEQ, D), 1.0),
        "c": nrm(ks[1], (BATCH, D), 1.0),
        "ctx": nrm(ks[2], (BATCH, CTX_LEN, D), 1.0),
        "c_ctx": nrm(ks[3], (D,), 1.0),
        "w_mod": nrm(ks[4], (DEPTH, D, 6 * D), 0.5 * D ** -0.5),
        "b_mod": nrm(ks[5], (DEPTH, 6 * D), 0.01),
        "g_mix": 1.0 + nrm(ks[6], (DEPTH, D), 0.05),
        "w_in": nrm(ks[7], (DEPTH, D, IN_WIDTH), D ** -0.5),
        "w_pool": nrm(ks[8], (DEPTH, 4, POOL_GROUP, POOL_GROUP), POOL_GROUP ** -0.5),
        "s_pool": 1.0 + nrm(ks[9], (DEPTH, POOL_WIDTH), 0.1),
        "w_gate": nrm(ks[10], (DEPTH, 2, GLA_GATE_RANK, GLA_HEADS * GLA_DK), GLA_GATE_RANK ** -0.5),
        "b_gate": nrm(ks[11], (DEPTH, 2, GLA_HEADS * GLA_DK), 0.1),
        "g_gla": 1.0 + nrm(ks[12], (DEPTH, GLA_DV), 0.05),
        "g_q": 1.0 + nrm(ks[13], (DEPTH, HEAD_DIM), 0.05),
        "g_k": 1.0 + nrm(ks[14], (DEPTH, HEAD_DIM), 0.05),
        "w_out": nrm(ks[15], (DEPTH, MIX_WIDTH, D), MIX_WIDTH ** -0.5),
        "g_mlp": 1.0 + nrm(ks[16], (DEPTH, D), 0.05),
        "w_up": nrm(ks[17], (DEPTH, D, MLP_HIDDEN), D ** -0.5),
        "w_down": nrm(ks[18], (DEPTH, MLP_HIDDEN, D), MLP_HIDDEN ** -0.5),
        "g_final": 1.0 + nrm(ks[19], (D,), 0.05),
    }


def reference(x, c, ctx, c_ctx, w_mod, b_mod, g_mix, w_in, w_pool, s_pool, w_gate, b_gate,
              g_gla, g_q, g_k, w_out, g_mlp, w_up, w_down, g_final):
    n_tok = x.shape[1]
    ROWS = n_tok // GRID_W
    rows = jnp.repeat(jnp.arange(ROWS, dtype=jnp.int32), GRID_W)
    cols = jnp.tile(jnp.arange(GRID_W, dtype=jnp.int32), ROWS)
    cx = ctx
    for l in range(DEPTH):
        x, cx = trunk_layer(x, cx, c, c_ctx, rows, cols, w_mod[l], b_mod[l], g_mix[l], w_in[l],
                            w_pool[l], s_pool[l], w_gate[l], b_gate[l], g_gla[l], g_q[l], g_k[l],
                            w_out[l], g_mlp[l], w_up[l], w_down[l], update_ctx=(l < DEPTH - 1))
    return rmsnorm(x, g_final)
```

```python
import functools

import numpy as np
import jax
import jax.numpy as jnp
from jax import lax
from jax.experimental import pallas as pl
from jax.experimental.pallas import tpu as pltpu

F32 = jnp.float32
BF16 = jnp.bfloat16

NORM_EPS = 1e-6
HEAD_DIM = 64
GRID_W = 64
ROPE_THETA = 10000.0
POOL_WINDOWS = (2, 4, 8, 16)
POOL_HALO = 8
GLA_CHUNK = 64
GLA_GATE_NORM = 16.0
N_LEVELS = 6
ROW_TILE = 256
ATTN_KV_CHUNK = 1280
VMEM_LIMIT = 56 * 1024 * 1024


def _cparams(sem):
    return pltpu.CompilerParams(dimension_semantics=sem, vmem_limit_bytes=VMEM_LIMIT)


def _split_hi_lo(a):
    hi = a.astype(BF16)
    lo = (a - hi.astype(F32)).astype(BF16)
    return hi, lo


def _nt_dot(a, b):
    return lax.dot_general(a, b, (((1,), (1,)), ((), ())), preferred_element_type=F32)


def _tn_dot(a, b):
    return lax.dot_general(a, b, (((0,), (0,)), ((), ())), preferred_element_type=F32)


def _segment_rms_scale(x, seg_ref, seg_t_ref):
    ssq = jnp.dot((x * x).astype(BF16), seg_ref[...], preferred_element_type=F32)
    rs = lax.rsqrt(ssq * (1.0 / HEAD_DIM) + NORM_EPS)
    hi, lo = _split_hi_lo(rs)
    return (jnp.dot(hi, seg_t_ref[...], preferred_element_type=F32)
            + jnp.dot(lo, seg_t_ref[...], preferred_element_type=F32))


def _mod_kernel(c_ref, w_ref, b_ref, o_ref):
    c = c_ref[...]
    s = (c * jax.nn.sigmoid(c)).astype(BF16)
    o_ref[...] = jnp.dot(s, w_ref[...].astype(BF16), preferred_element_type=F32) + b_ref[...]


def _modulation(cvec8, w_mod, b_mod):
    depth, d, d6 = w_mod.shape
    nb = 4
    bw = d6 // nb
    return pl.pallas_call(
        _mod_kernel,
        name="modulation",
        grid=(depth, nb),
        in_specs=[pl.BlockSpec((8, d), lambda l, j: (0, 0)),
                  pl.BlockSpec((None, d, bw), lambda l, j: (l, 0, j)),
                  pl.BlockSpec((None, 1, bw), lambda l, j: (l, 0, j))],
        out_specs=pl.BlockSpec((None, 8, bw), lambda l, j: (l, 0, j)),
        out_shape=jax.ShapeDtypeStruct((depth, 8, d6), F32),
        compiler_params=_cparams(("arbitrary", "arbitrary")),
    )(cvec8, w_mod, b_mod.reshape(depth, 1, d6))


def _rope(x, cos, sin_signed):
    w = x.shape[1]
    lane = lax.broadcasted_iota(jnp.int32, x.shape, 1)
    partner = jnp.where((lane % 32) < 16, pltpu.roll(x, w - 16, 1), pltpu.roll(x, 16, 1))
    return x * cos + partner * sin_signed


def _in_kernel(x_ref, mod_ref, gmix_ref, w_ref, wg_ref, bg_ref, gq_ref, gk_ref, seg_ref, seg_t_ref,
               cos_ref, sin_ref, p_ref, gl_ref, la_ref, aq_ref, ak_ref, av_ref):
    x = x_ref[...]
    mod = mod_ref[...]
    sh1, sc1 = mod[0:1], mod[1:2]
    h = x * lax.rsqrt(jnp.mean(x * x, axis=-1, keepdims=True) + NORM_EPS) * gmix_ref[...]
    h = h * (1.0 + sc1) + sh1
    u = jnp.dot(h.astype(BF16), w_ref[...], preferred_element_type=F32)
    p_ref[...] = u[:, 0:256]
    gl_ref[...] = u[:, 256:1280]
    z = jnp.dot(u[:, 1280:1408].astype(BF16), wg_ref[...], preferred_element_type=F32) + bg_ref[...]
    la_ref[...] = (jnp.minimum(z, 0.0) - jnp.log1p(jnp.exp(-jnp.abs(z)))) * (1.0 / GLA_GATE_NORM)
    q, k, v = u[:, 1408:1920], u[:, 1920:2048], u[:, 2048:2176]
    cos, sin = cos_ref[...], sin_ref[...]
    qn = q * _segment_rms_scale(q, seg_ref, seg_t_ref) * gq_ref[...]
    qr = _rope(qn, jnp.concatenate([cos] * 4, axis=1), jnp.concatenate([sin] * 4, axis=1))
    aq_ref[...] = (qr * (HEAD_DIM ** -0.5)).astype(BF16)
    kn = k * _segment_rms_scale(k, seg_ref.at[0:128], seg_t_ref.at[:, 0:128]) * gk_ref[...]
    kr = _rope(kn, cos, sin)
    ak_ref[0] = kr[:, 0:64].astype(BF16)
    ak_ref[1] = kr[:, 64:128].astype(BF16)
    av_ref[0] = v[:, 0:64].astype(BF16)
    av_ref[1] = v[:, 64:128].astype(BF16)


def _in_proj(xa, mod, gmix, w_cat, wg_blk, bg_cat, gq_t, gk_t, seg, seg_t, cos_t, sin_t):
    bsz, s, d = xa.shape
    nt = s // ROW_TILE
    t = ROW_TILE
    wcols = w_cat.shape[1]
    const = lambda shape: pl.BlockSpec(shape, lambda b, i: (0,) * len(shape))
    return pl.pallas_call(
        _in_kernel,
        name="in_proj",
        grid=(bsz, nt),
        in_specs=[pl.BlockSpec((None, t, d), lambda b, i: (b, i, 0)),
                  pl.BlockSpec((None, 6, d), lambda b, i: (jnp.where(i == 0, bsz, b), 0, 0)),
                  const((1, d)), const((d, wcols)), const((128, 512)), const((1, 512)),
                  const((1, 512)), const((1, 128)), const((512, 128)), const((128, 512)),
                  pl.BlockSpec((t, 128), lambda b, i: (i, 0)),
                  pl.BlockSpec((t, 128), lambda b, i: (i, 0))],
        out_specs=[pl.BlockSpec((None, t, 256), lambda b, i: (b, i, 0)),
                   pl.BlockSpec((None, t, 1024), lambda b, i: (b, i, 0)),
                   pl.BlockSpec((None, t, 512), lambda b, i: (b, i, 0)),
                   pl.BlockSpec((None, t, 512), lambda b, i: (b, i, 0)),
                   pl.BlockSpec((None, 2, t, 64), lambda b, i: (b, 0, i, 0)),
                   pl.BlockSpec((None, 2, t, 64), lambda b, i: (b, 0, i, 0))],
        out_shape=[jax.ShapeDtypeStruct((bsz, s, 256), F32),
                   jax.ShapeDtypeStruct((bsz, s, 1024), F32),
                   jax.ShapeDtypeStruct((bsz, s, 512), F32),
                   jax.ShapeDtypeStruct((bsz, s, 512), BF16),
                   jax.ShapeDtypeStruct((bsz, 2, s, 64), BF16),
                   jax.ShapeDtypeStruct((bsz, 2, s, 64), BF16)],
        compiler_params=_cparams(("arbitrary", "arbitrary")),
    )(xa, mod, gmix, w_cat, wg_blk, bg_cat, gq_t, gk_t, seg, seg_t, cos_t, sin_t)


def _pool_kernel(prev_ref, cur_ref, next_ref, wbd_ref, sp_ref, o_ref, ext_sc, *, n_tiles, n_latent):
    i = pl.program_id(1)
    t = ROW_TILE
    hl = POOL_HALO
    has_prev = i >= 2
    has_next = jnp.logical_and(i >= 1, i < n_tiles - 1)
    cur = cur_ref[...]
    ext_sc[0:hl] = jnp.where(has_prev, prev_ref[...], 0.0)
    ext_sc[hl:hl + t] = cur
    ext_sc[hl + t:hl + t + hl] = jnp.where(has_next, next_ref[...], 0.0)

    def shifted(off):
        return ext_sc[hl + off:hl + off + t]

    a2 = shifted(-1) + cur
    a4 = a2 + shifted(-2) + shifted(1)
    a8 = a4 + shifted(-4) + shifted(-3) + shifted(2) + shifted(3)
    a16 = a8
    for off in (-8, -7, -6, -5, 4, 5, 6, 7):
        a16 = a16 + shifted(off)
    row = lax.broadcasted_iota(jnp.int32, (t, 256), 0)
    lane = lax.broadcasted_iota(jnp.int32, (t, 256), 1)
    pos = row + jnp.where(i == 0, 0, (i - 1) * t)
    seg_len = jnp.where(i == 0, t, n_latent)

    def mean(acc, w):
        lo = jnp.maximum(pos - w // 2, 0)
        hi = jnp.minimum(pos + w // 2 - 1, seg_len - 1)
        return acc / (hi - lo + 1).astype(F32)

    m = jnp.where(lane < 64, mean(a2, 2),
                  jnp.where(lane < 128, mean(a4, 4), jnp.where(lane < 192, mean(a8, 8), mean(a16, 16))))
    m = m - cur
    y = jnp.dot(m.astype(BF16), wbd_ref[...], preferred_element_type=F32) * sp_ref[...]
    o_ref[...] = y.astype(BF16)


def _pool(p, wbd, sp):
    bsz, s, w = p.shape
    t = ROW_TILE
    nt = s // t
    hb = t // POOL_HALO
    last_hblk = s // POOL_HALO - 1
    kern = functools.partial(_pool_kernel, n_tiles=nt, n_latent=s - t)
    return pl.pallas_call(
        kern,
        name="pool_mixer",
        grid=(bsz, nt),
        in_specs=[pl.BlockSpec((None, POOL_HALO, w), lambda b, i: (b, jnp.maximum(i * hb - 1, 0), 0)),
                  pl.BlockSpec((None, t, w), lambda b, i: (b, i, 0)),
                  pl.BlockSpec((None, POOL_HALO, w), lambda b, i: (b, jnp.minimum((i + 1) * hb, last_hblk), 0)),
                  pl.BlockSpec((w, w), lambda b, i: (0, 0)),
                  pl.BlockSpec((1, w), lambda b, i: (0, 0))],
        out_specs=pl.BlockSpec((None, t, w), lambda b, i: (b, i, 0)),
        out_shape=jax.ShapeDtypeStruct((bsz, s, w), BF16),
        scratch_shapes=[pltpu.VMEM((t + 2 * POOL_HALO, w), F32)],
        compiler_params=_cparams(("arbitrary", "arbitrary")),
    )(p, p, p, wbd, sp)


def _gla_constants():
    c = GLA_CHUNK
    cm = np.zeros((2, (N_LEVELS + 1) * c, c), np.float32)
    up = np.zeros((2, N_LEVELS, c, 1), np.float32)
    bm = np.zeros((2, N_LEVELS + 1, c, c), np.float32)
    idx = np.arange(c)
    cm0 = np.zeros(((N_LEVELS + 1) * c, c), np.float32)
    cm0[0:c] = (idx[None, :] <= idx[:, None])
    up0 = np.zeros((N_LEVELS, c), np.float32)
    bm0 = np.zeros((N_LEVELS + 1, c, c), np.float32)
    for lv in range(N_LEVELS):
        s = (c // 2) >> lv
        for i in range(c):
            bnd = (i // (2 * s)) * 2 * s + s
            if i >= bnd:
                cm0[(lv + 1) * c + i, bnd:i + 1] = 1.0
                up0[lv, i] = 1.0
                bm0[lv, i, bnd - s:bnd] = 1.0
            else:
                cm0[(lv + 1) * c + i, i + 1:bnd] = 1.0
    bm0[N_LEVELS] = np.eye(c)
    cm[0], up[0, :, :, 0], bm[0] = cm0, up0, bm0
    cm[1] = cm0.reshape(N_LEVELS + 1, c, c)[:, ::-1, ::-1].reshape(-1, c)
    up[1, :, :, 0] = up0[:, ::-1]
    bm[1, :N_LEVELS] = bm0[:N_LEVELS, ::-1, ::-1]
    upf = np.broadcast_to(up, (2, N_LEVELS, c, 256)).copy()
    bmf = np.tile(bm, (1, 1, 1, 4))
    bd = np.kron(np.eye(4, dtype=np.float32), np.ones((64, 64), np.float32))
    return cm, upf, bmf, bd


def _gla_kernel(qkv_ref, la_ref, cm_ref, up_ref, bm_ref, bd_ref, o_ref, st_sc):
    d = pl.program_id(1)
    i = pl.program_id(2)
    c = GLA_CHUNK

    @pl.when(i == 0)
    def _():
        st_sc[...] = jnp.zeros_like(st_sc)

    bd = bd_ref[...]
    for cc in range(ROW_TILE // c):
        ci = jnp.where(d == 0, cc, ROW_TILE // c - 1 - cc)
        start = pl.multiple_of(ci * c, c)
        q = qkv_ref[pl.ds(start, c), 0:256] * (HEAD_DIM ** -0.5)
        k = qkv_ref[pl.ds(start, c), 256:512]
        v = qkv_ref[pl.ds(start, c), 512:768]
        la = la_ref[pl.ds(start, c), :]
        hi, lo = _split_hi_lo(la)
        cums = jnp.dot(cm_ref[...], jnp.concatenate([hi, lo], axis=1), preferred_element_type=F32)
        dall = cums[:, 0:256] + cums[:, 256:512]
        b = dall[0:c]
        btot = jnp.sum(la, axis=0, keepdims=True)
        st = st_sc[...]
        o = _nt_dot((q * jnp.exp(b)).astype(BF16), st.astype(BF16))
        a = jnp.zeros((c, 256), F32)
        for lv in range(N_LEVELS + 1):
            if lv < N_LEVELS:
                e = jnp.exp(dall[(lv + 1) * c:(lv + 2) * c])
                upm = up_ref[lv]
                qt = q * e * upm
                kt = k * e * (1.0 - upm)
            else:
                qt, kt = q, k
            y = (jnp.concatenate([kt] * 4, axis=0) * bd).astype(BF16)
            a = a + _nt_dot(qt.astype(BF16), y) * bm_ref[lv]
        vbd = (jnp.concatenate([v] * 4, axis=0) * bd).astype(BF16)
        o = o + jnp.dot(a.astype(BF16), vbd, preferred_element_type=F32)
        o_ref[pl.ds(start, c), :] = o
        kdec = k * jnp.exp(btot - b)
        st_sc[...] = st * jnp.exp(btot) + _tn_dot(v.astype(BF16), kdec.astype(BF16)) * bd


def _gla(gl, la, consts):
    bsz, s, _ = gl.shape
    t = ROW_TILE
    nt = s // t
    cm, upf, bmf, bd = consts

    def tile(d, i):
        return jnp.where(d == 0, i, jnp.where(i == 0, 0, nt - i))

    return pl.pallas_call(
        _gla_kernel,
        name="gla_scan",
        grid=(bsz, 2, nt),
        in_specs=[pl.BlockSpec((None, t, 768), lambda b, d, i: (b, tile(d, i), 0)),
                  pl.BlockSpec((None, t, 256), lambda b, d, i: (b, tile(d, i), d)),
                  pl.BlockSpec((None,) + cm.shape[1:], lambda b, d, i: (d, 0, 0)),
                  pl.BlockSpec((None,) + upf.shape[1:], lambda b, d, i: (d, 0, 0, 0)),
                  pl.BlockSpec((None,) + bmf.shape[1:], lambda b, d, i: (d, 0, 0, 0)),
                  pl.BlockSpec(bd.shape, lambda b, d, i: (0, 0))],
        out_specs=pl.BlockSpec((None, None, t, 256), lambda b, d, i: (d, b, tile(d, i), 0)),
        out_shape=jax.ShapeDtypeStruct((2, bsz, s, 256), F32),
        scratch_shapes=[pltpu.VMEM((256, 256), F32)],
        compiler_params=_cparams(("arbitrary", "arbitrary", "arbitrary")),
    )(gl, la, cm, upf, bmf, bd)


def _attn_kernel(q_ref, k_ref, v_ref, o_ref, m_sc, l_sc, acc_sc, *, kv_len, kv_chunk):
    qi = pl.program_id(2)
    groups = 4
    m_sc[...] = jnp.full_like(m_sc, -jnp.inf)
    l_sc[...] = jnp.zeros_like(l_sc)
    acc_sc[...] = jnp.zeros_like(acc_sc)
    qs = [q_ref[:, g * HEAD_DIM:(g + 1) * HEAD_DIM] for g in range(groups)]

    def process(start, size):
        kc = k_ref[pl.ds(start, size), :]
        vc = v_ref[pl.ds(start, size), :]
        for g in range(groups):
            s = _nt_dot(qs[g], kc)
            m_prev = m_sc[g]
            m_new = jnp.maximum(m_prev, jnp.max(s, axis=-1, keepdims=True))
            alpha = jnp.exp(m_prev - m_new)
            p = jnp.exp(s - m_new)
            l_sc[g] = alpha * l_sc[g] + jnp.sum(p, axis=-1, keepdims=True)
            acc_sc[g] = alpha * acc_sc[g] + jnp.dot(p.astype(BF16), vc, preferred_element_type=F32)
            m_sc[g] = m_new

    @pl.when(qi == 0)
    def _():
        process(0, ROW_TILE)

    @pl.when(qi > 0)
    def _():
        def body(j, carry):
            process(pl.multiple_of(j * kv_chunk, kv_chunk), kv_chunk)
            return carry
        lax.fori_loop(0, kv_len // kv_chunk, body, 0)

    for g in range(groups):
        o_ref[:, g * HEAD_DIM:(g + 1) * HEAD_DIM] = (acc_sc[g] / l_sc[g]).astype(BF16)


def _attention(aq, ak, av, kv_chunk):
    bsz, s, _ = aq.shape
    t = ROW_TILE
    nt = s // t
    kern = functools.partial(_attn_kernel, kv_len=s, kv_chunk=kv_chunk)
    return pl.pallas_call(
        kern,
        name="gqa_attention",
        grid=(bsz, 2, nt),
        in_specs=[pl.BlockSpec((None, t, 256), lambda b, h, i: (b, i, h)),
                  pl.BlockSpec((None, None, s, HEAD_DIM), lambda b, h, i: (b, h, 0, 0)),
                  pl.BlockSpec((None, None, s, HEAD_DIM), lambda b, h, i: (b, h, 0, 0))],
        out_specs=pl.BlockSpec((None, t, 256), lambda b, h, i: (b, i, h)),
        out_shape=jax.ShapeDtypeStruct((bsz, s, 512), BF16),
        scratch_shapes=[pltpu.VMEM((4, t, 1), F32), pltpu.VMEM((4, t, 1), F32),
                        pltpu.VMEM((4, t, HEAD_DIM), F32)],
        compiler_params=_cparams(("arbitrary", "arbitrary", "arbitrary")),
    )(aq, ak, av)


def _out_kernel(x_ref, mod_ref, pool_ref, of_ref, ob_ref, gg_ref, at_ref, ggla_ref, seg_ref, seg_t_ref,
                wout_ref, gmlp_ref, wup_ref, wdn_ref, o_ref):
    x = x_ref[...]
    mod = mod_ref[...]
    ga1, sh2, sc2, ga2 = mod[2:3], mod[3:4], mod[4:5], mod[5:6]
    og = of_ref[...] + ob_ref[...]
    gg = gg_ref[...]
    gla = og * _segment_rms_scale(og, seg_ref.at[0:256], seg_t_ref.at[:, 0:256]) * ggla_ref[...]
    gla = (gla * (gg * jax.nn.sigmoid(gg))).astype(BF16)
    mix = (jnp.dot(pool_ref[...], wout_ref[0:256], preferred_element_type=F32)
           + jnp.dot(gla, wout_ref[256:512], preferred_element_type=F32)
           + jnp.dot(at_ref[...], wout_ref[512:1024], preferred_element_type=F32))
    x1 = x + ga1 * mix
    h = x1 * lax.rsqrt(jnp.mean(x1 * x1, axis=-1, keepdims=True) + NORM_EPS) * gmlp_ref[...]
    h = (h * (1.0 + sc2) + sh2).astype(BF16)
    hidden = wup_ref.shape[1]
    hc = 1024
    y = jnp.zeros_like(x)
    for j in range(hidden // hc):
        a = jnp.dot(h, wup_ref[:, j * hc:(j + 1) * hc], preferred_element_type=F32)
        a = jnp.square(jnp.maximum(a, 0.0)).astype(BF16)
        y = y + jnp.dot(a, wdn_ref[j * hc:(j + 1) * hc, :], preferred_element_type=F32)
    o_ref[...] = x1 + ga2 * y


def _out_mlp(xa, mod, pool_o, gla_o, gl, attn_o, ggla_t, seg, seg_t, w_out, gmlp, w_up, w_dn):
    bsz, s, d = xa.shape
    t = ROW_TILE
    nt = s // t
    const = lambda shape: pl.BlockSpec(shape, lambda b, i: (0,) * len(shape))
    return pl.pallas_call(
        _out_kernel,
        name="out_mlp",
        grid=(bsz, nt),
        in_specs=[pl.BlockSpec((None, t, d), lambda b, i: (b, i, 0)),
                  pl.BlockSpec((None, 6, d), lambda b, i: (jnp.where(i == 0, bsz, b), 0, 0)),
                  pl.BlockSpec((None, t, 256), lambda b, i: (b, i, 0)),
                  pl.BlockSpec((None, None, t, 256), lambda b, i: (0, b, i, 0)),
                  pl.BlockSpec((None, None, t, 256), lambda b, i: (1, b, i, 0)),
                  pl.BlockSpec((None, t, 256), lambda b, i: (b, i, 3)),
                  pl.BlockSpec((None, t, 512), lambda b, i: (b, i, 0)),
                  const((1, 256)), const((512, 128)), const((128, 512)),
                  const(w_out.shape), const((1, d)), const(w_up.shape), const(w_dn.shape)],
        out_specs=pl.BlockSpec((None, t, d), lambda b, i: (b, i, 0)),
        out_shape=jax.ShapeDtypeStruct((bsz, s, d), F32),
        compiler_params=_cparams(("arbitrary", "arbitrary")),
    )(xa, mod, pool_o, gla_o, gla_o, gl, attn_o, ggla_t, seg, seg_t, w_out, gmlp, w_up, w_dn)


def _final_kernel(x_ref, g_ref, o_ref):
    x = x_ref[...]
    o_ref[...] = x * lax.rsqrt(jnp.mean(x * x, axis=-1, keepdims=True) + NORM_EPS) * g_ref[...]


def _final_norm(xa, g_final, n_latent):
    bsz, s, d = xa.shape
    t = ROW_TILE
    return pl.pallas_call(
        _final_kernel,
        name="final_norm",
        grid=(bsz, n_latent // t),
        in_specs=[pl.BlockSpec((None, t, d), lambda b, i: (b, i + 1, 0)),
                  pl.BlockSpec((1, d), lambda b, i: (0, 0))],
        out_specs=pl.BlockSpec((None, t, d), lambda b, i: (b, i, 0)),
        out_shape=jax.ShapeDtypeStruct((bsz, n_latent, d), F32),
        compiler_params=_cparams(("arbitrary", "arbitrary")),
    )(xa, g_final.reshape(1, d))


def _rope_tables(n_latent, n_ctx):
    half = HEAD_DIM // 2
    inv = ROPE_THETA ** (-np.arange(0, half, 2, dtype=np.float32) / half)
    t = np.arange(n_latent)
    ang_r = (t // GRID_W).astype(np.float32)[:, None] * inv
    ang_c = (t % GRID_W).astype(np.float32)[:, None] * inv
    cos = np.concatenate([np.cos(ang_r), np.cos(ang_r), np.cos(ang_c), np.cos(ang_c)], axis=1)
    sin = np.concatenate([-np.sin(ang_r), np.sin(ang_r), -np.sin(ang_c), np.sin(ang_c)], axis=1)
    cos = np.concatenate([np.ones((n_ctx, HEAD_DIM), np.float32), cos.astype(np.float32)], axis=0)
    sin = np.concatenate([np.zeros((n_ctx, HEAD_DIM), np.float32), sin.astype(np.float32)], axis=0)
    return np.tile(cos, (1, 2)), np.tile(sin, (1, 2))


def _pick_kv_chunk(s):
    for cand in (ATTN_KV_CHUNK, 1024, 768, 512, 256):
        if s % cand == 0:
            return cand
    raise ValueError(f"unsupported sequence length {s}")


@jax.jit
def kernel(x, c, ctx, c_ctx, w_mod, b_mod, g_mix, w_in, w_pool, s_pool, w_gate, b_gate, g_gla, g_q, g_k,
           w_out, g_mlp, w_up, w_down, g_final):
    bsz, n_latent, d = x.shape
    n_ctx = ctx.shape[1]
    depth = w_mod.shape[0]
    assert n_ctx == ROW_TILE and n_latent % ROW_TILE == 0 and d == 1024
    s = n_ctx + n_latent

    xa = jnp.concatenate([ctx, x], axis=1)
    cvec8 = jnp.zeros((8, d), F32).at[0:bsz].set(c).at[bsz].set(c_ctx)
    mods = _modulation(cvec8, w_mod, b_mod)[:, :bsz + 1].reshape(depth, bsz + 1, 6, d)

    cos_t, sin_t = _rope_tables(n_latent, n_ctx)
    seg_np = np.zeros((512, 128), np.float32)
    seg_np[np.arange(512), np.arange(512) // HEAD_DIM] = 1.0
    seg = jnp.asarray(seg_np, BF16)
    seg_t = jnp.asarray(seg_np.T, BF16)
    cm, upf, bmf, bd = _gla_constants()
    gla_consts = (jnp.asarray(cm, BF16), jnp.asarray(upf), jnp.asarray(bmf), jnp.asarray(bd))
    kv_chunk = _pick_kv_chunk(s)

    pad_r = jnp.zeros((depth, d, 96), F32)
    w_cat = jnp.concatenate([w_in[:, :, 0:1280], w_in[:, :, 1280:1312], pad_r, w_in[:, :, 1312:]],
                            axis=2).astype(BF16)
    wg_blk = jnp.zeros((depth, 128, 512), F32)
    wg_blk = wg_blk.at[:, 0:16, 0:256].set(w_gate[:, 0]).at[:, 16:32, 256:512].set(w_gate[:, 1]).astype(BF16)
    bg_cat = b_gate.reshape(depth, 1, 512)
    wbd = jnp.zeros((depth, 256, 256), F32)
    for g in range(4):
        wbd = wbd.at[:, g * 64:(g + 1) * 64, g * 64:(g + 1) * 64].set(w_pool[:, g])
    wbd = wbd.astype(BF16)
    w_out_b, w_up_b, w_dn_b = w_out.astype(BF16), w_up.astype(BF16), w_down.astype(BF16)

    for l in range(depth):
        p, gl, la, aq, ak, av = _in_proj(
            xa, mods[l], g_mix[l].reshape(1, d), w_cat[l], wg_blk[l], bg_cat[l],
            jnp.tile(g_q[l], 8).reshape(1, 512), jnp.tile(g_k[l], 2).reshape(1, 128), seg, seg_t,
            jnp.asarray(cos_t), jnp.asarray(sin_t))
        pool_o = _pool(p, wbd[l], s_pool[l].reshape(1, 256))
        gla_o = _gla(gl, la, gla_consts)
        attn_o = _attention(aq, ak, av, kv_chunk)
        xa = _out_mlp(xa, mods[l], pool_o, gla_o, gl, attn_o, jnp.tile(g_gla[l], 4).reshape(1, 256),
                      seg, seg_t, w_out_b[l], g_mlp[l].reshape(1, d), w_up_b[l], w_dn_b[l])
    return _final_norm(xa, g_final, n_latent)
```
